```python
import jax, jax.numpy as jnp
from jax import lax
import numpy as np

D_MODEL = 1024
BATCH = 4
SEQ = 4096
DEPTH = 1

N_META = 16
BLOCK = 128
PAD_FRONT = BLOCK - N_META
N_Q_HEADS = 8
N_KV_HEADS = 2
HEAD_DIM = 64
WINDOW = 128
ROPE_DIM = HEAD_DIM // 4
ROPE_THETA = 500000.0
N_R_HEADS = 4
R_KEY_DIM = 128
R_VAL_DIM = 128
CHUNK = 64
SUB = 16
ATTN_WIDTH = N_Q_HEADS * HEAD_DIM
KV_WIDTH = N_KV_HEADS * HEAD_DIM
REC_K_WIDTH = N_R_HEADS * R_KEY_DIM
REC_V_WIDTH = N_R_HEADS * R_VAL_DIM
SPLIT_WIDTHS = (ATTN_WIDTH, KV_WIDTH, KV_WIDTH, REC_K_WIDTH, REC_K_WIDTH, REC_V_WIDTH, REC_V_WIDTH, D_MODEL, D_MODEL)
IN_WIDTH = sum(SPLIT_WIDTHS)
N_SUBKEYS = 128
N_EXPERTS = N_SUBKEYS * N_SUBKEYS
PEER_HEADS = 8
PEER_QUERY_DIM = 128
PEER_TOPK = 16
PEER_BLOCK = 256
EPS = 1e-6
MASK_VALUE = -1e30

kernel_name = 'hybrid_swa_hgrn2_peer_block'


def rmsnorm(x, g):
    xf = x.astype(jnp.float32)
    y = xf * lax.rsqrt(jnp.mean(xf * xf, axis=-1, keepdims=True) + EPS)
    return (y * g.astype(jnp.float32)).astype(x.dtype)


def apply_partial_rope(t, pos):
    half = ROPE_DIM // 2
    inv_freq = jnp.power(ROPE_THETA, -jnp.arange(half, dtype=jnp.float32) * 2.0 / ROPE_DIM)
    ang = pos.astype(jnp.float32)[:, None] * inv_freq[None, :]
    cos = jnp.cos(ang)[None, :, None, :]
    sin = jnp.sin(ang)[None, :, None, :]
    tf = t.astype(jnp.float32)
    t1, t2 = tf[..., :half], tf[..., half:ROPE_DIM]
    out = jnp.concatenate([t1 * cos - t2 * sin, t2 * cos + t1 * sin, tf[..., ROPE_DIM:]], axis=-1)
    return out.astype(t.dtype)


def sliding_window_attention(q, k, v, sinks):
    B, Lp = q.shape[0], q.shape[1]
    nb = Lp // BLOCK
    G = N_Q_HEADS // N_KV_HEADS
    qb = q.astype(jnp.float32).reshape(B, nb, BLOCK, N_KV_HEADS, G, HEAD_DIM)
    kb = k.astype(jnp.float32).reshape(B, nb, BLOCK, N_KV_HEADS, HEAD_DIM)
    vb = v.astype(jnp.float32).reshape(B, nb, BLOCK, N_KV_HEADS, HEAD_DIM)
    k_band = jnp.concatenate([jnp.concatenate([jnp.zeros_like(kb[:, :1]), kb[:, :-1]], axis=1), kb], axis=2)
    v_band = jnp.concatenate([jnp.concatenate([jnp.zeros_like(vb[:, :1]), vb[:, :-1]], axis=1), vb], axis=2)
    meta_k = k[:, PAD_FRONT:BLOCK].astype(jnp.float32)
    meta_v = v[:, PAD_FRONT:BLOCK].astype(jnp.float32)
    scale = HEAD_DIM ** -0.5
    s_band = jnp.einsum('bnqhgd,bnkhd->bnhgqk', qb, k_band) * scale
    s_meta = jnp.einsum('bnqhgd,bmhd->bnhgqm', qb, meta_k) * scale
    p_q = jnp.arange(Lp).reshape(nb, BLOCK)
    p_k = p_q[:, :1] - BLOCK + jnp.arange(2 * BLOCK)[None, :]
    dq = p_q[:, :, None] - p_k[:, None, :]
    band_ok = (dq >= 0) & (dq < WINDOW) & (p_k[:, None, :] >= BLOCK)
    meta_ok = (PAD_FRONT + jnp.arange(N_META))[None, None, :] <= p_q[:, :, None]
    s_band = jnp.where(band_ok[None, :, None, None], s_band, MASK_VALUE)
    s_meta = jnp.where(meta_ok[None, :, None, None], s_meta, MASK_VALUE)
    sink = jnp.broadcast_to(sinks.astype(jnp.float32).reshape(1, 1, N_KV_HEADS, G, 1, 1), s_band.shape[:-1] + (1,))
    probs = jax.nn.softmax(jnp.concatenate([s_band, s_meta, sink], axis=-1), axis=-1)
    p_band = probs[..., :2 * BLOCK]
    p_meta = probs[..., 2 * BLOCK:2 * BLOCK + N_META]
    out = (jnp.einsum('bnhgqk,bnkhd->bnqhgd', p_band, v_band)
           + jnp.einsum('bnhgqm,bmhd->bnqhgd', p_meta, meta_v))
    return out.reshape(B, Lp, N_Q_HEADS * HEAD_DIM)


def hgrn2_chunkwise(q, k, v, log_f):
    B, Lp, H, dk = q.shape
    dv = v.shape[-1]
    nc = Lp // CHUNK
    ns = CHUNK // SUB

    def to_chunks(t):
        return t.astype(jnp.float32).reshape(B, nc, CHUNK, H, t.shape[-1]).transpose(0, 3, 1, 2, 4)

    qc, kc, vc, lf = to_chunks(q), to_chunks(k), to_chunks(v), to_chunks(log_f)
    b = jnp.cumsum(lf, axis=3)
    b_last = b[:, :, :, -1]
    dS = jnp.einsum('bhncd,bhnce->bhnde', kc * jnp.exp(b_last[:, :, :, None] - b), vc)
    decay = jnp.exp(b_last)

    def step(S, inp):
        a, u = inp
        return a[..., None] * S + u, S

    _, S_prev = lax.scan(step, jnp.zeros((B, H, dk, dv), jnp.float32),
                         (decay.transpose(2, 0, 1, 3), dS.transpose(2, 0, 1, 3, 4)))
    S_prev = S_prev.transpose(1, 2, 0, 3, 4)
    o_inter = jnp.einsum('bhncd,bhnde->bhnce', qc * jnp.exp(b), S_prev)
    qs = qc.reshape(B, H, nc, ns, SUB, dk)
    ks = kc.reshape(B, H, nc, ns, SUB, dk)
    vs = vc.reshape(B, H, nc, ns, SUB, dv)
    bs = b.reshape(B, H, nc, ns, SUB, dk)
    b_ref = jnp.concatenate([jnp.zeros_like(bs[:, :, :, :1, 0]), bs[:, :, :, :-1, -1]], axis=3)
    q_t = qs * jnp.exp(bs - b_ref[:, :, :, :, None, :])
    off_mask = jnp.arange(ns)[None, :] < jnp.arange(ns)[:, None]
    expo_off = b_ref[:, :, :, :, None, None, :] - bs[:, :, :, None]
    k_t = ks[:, :, :, None] * jnp.exp(jnp.where(off_mask[:, :, None, None], expo_off, MASK_VALUE))
    a_off = jnp.einsum('bhnitd,bhnijsd->bhnitjs', q_t, k_t)
    o_off = jnp.einsum('bhnitjs,bhnjse->bhnite', a_off, vs)
    causal = jnp.tril(jnp.ones((SUB, SUB), dtype=bool))
    expo_diag = bs[..., :, None, :] - bs[..., None, :, :]
    dec_diag = jnp.exp(jnp.where(causal[:, :, None], expo_diag, MASK_VALUE))
    a_diag = jnp.einsum('bhnitd,bhnisd,bhnitsd->bhnits', qs, ks, dec_diag)
    o_diag = jnp.einsum('bhnits,bhnise->bhnite', a_diag, vs)
    o = o_inter + (o_off + o_diag).reshape(B, H, nc, CHUNK, dv)
    return o.transpose(0, 2, 3, 1, 4).reshape(B, Lp, H, dv)


def hybrid_mixer(xn, w_in, b_in, sinks, lower_bound, rec_norm_g, w_up_attn, w_up_rec, w_out, pos):
    B, L, _ = xn.shape
    Lp = L + PAD_FRONT
    xp = jnp.pad(xn, ((0, 0), (PAD_FRONT, 0), (0, 0)))
    proj = xp @ w_in + b_in
    q, k, v, rq, rf, ri, rg, ga, gr = jnp.split(proj, np.cumsum(SPLIT_WIDTHS)[:-1].tolist(), axis=-1)
    q = apply_partial_rope(q.reshape(B, Lp, N_Q_HEADS, HEAD_DIM), pos)
    k = apply_partial_rope(k.reshape(B, Lp, N_KV_HEADS, HEAD_DIM), pos)
    v = v.reshape(B, Lp, N_KV_HEADS, HEAD_DIM)
    attn_out = sliding_window_attention(q, k, v, sinks)[:, PAD_FRONT:].astype(xn.dtype)
    lb = lower_bound.reshape(N_R_HEADS, R_KEY_DIM)
    valid = (jnp.arange(Lp) >= PAD_FRONT)[None, :, None, None]
    z_f = rf.reshape(B, Lp, N_R_HEADS, R_KEY_DIM).astype(jnp.float32)
    log_f = jnp.logaddexp(jnp.log(lb), jnp.log1p(-lb) + jax.nn.log_sigmoid(z_f))
    log_f = jnp.where(valid, log_f, 0.0)
    k_r = -jnp.expm1(log_f)
    o_r = hgrn2_chunkwise(rq.reshape(B, Lp, N_R_HEADS, R_KEY_DIM), k_r,
                          ri.reshape(B, Lp, N_R_HEADS, R_VAL_DIM), log_f)[:, PAD_FRONT:]
    o_r = o_r * lax.rsqrt(jnp.mean(o_r * o_r, axis=-1, keepdims=True) + EPS)
    o_r = o_r * rec_norm_g.astype(jnp.float32).reshape(N_R_HEADS, R_VAL_DIM)
    o_r = o_r * jax.nn.silu(rg[:, PAD_FRONT:].astype(jnp.float32).reshape(B, L, N_R_HEADS, R_VAL_DIM))
    rec_out = o_r.reshape(B, L, REC_V_WIDTH).astype(xn.dtype)
    merged = (jax.nn.sigmoid(ga[:, PAD_FRONT:]) * (attn_out @ w_up_attn)
              + jax.nn.sigmoid(gr[:, PAD_FRONT:]) * (rec_out @ w_up_rec))
    return merged @ w_out


def peer_ffn(x, w_query, sub_keys, expert_down, expert_up):
    T = x.shape[0]
    T_pad = -(-T // PEER_BLOCK) * PEER_BLOCK
    xp = jnp.pad(x, ((0, T_pad - T), (0, 0)))
    half = PEER_QUERY_DIM // 2
    qry = (xp @ w_query).astype(jnp.float32).reshape(T_pad, PEER_HEADS, 2, half)
    scores = jnp.einsum('thpd,hpnd->thpn', qry, sub_keys.astype(jnp.float32))
    s_top, i_top = lax.top_k(scores, PEER_TOPK)
    cand = s_top[:, :, 0, :, None] + s_top[:, :, 1, None, :]
    cand_idx = i_top[:, :, 0, :, None] * N_SUBKEYS + i_top[:, :, 1, None, :]
    best, sel = lax.top_k(cand.reshape(T_pad, PEER_HEADS, PEER_TOPK * PEER_TOPK), PEER_TOPK)
    idx = jnp.take_along_axis(cand_idx.reshape(T_pad, PEER_HEADS, PEER_TOPK * PEER_TOPK), sel, axis=-1)
    gate = jax.nn.softmax(best, axis=-1)
    nblk = T_pad // PEER_BLOCK

    def apply_block(args):
        xb, ib, wb = args
        u = expert_down[ib]
        act = jax.nn.gelu(jnp.einsum('td,thkd->thk', xb, u).astype(jnp.float32), approximate=False) * wb
        return jnp.einsum('thk,thkd->td', act.astype(xb.dtype), expert_up[ib])

    y = lax.map(apply_block, (xp.reshape(nblk, PEER_BLOCK, D_MODEL),
                              idx.reshape(nblk, PEER_BLOCK, PEER_HEADS, PEER_TOPK),
                              gate.reshape(nblk, PEER_BLOCK, PEER_HEADS, PEER_TOPK)))
    return y.reshape(T_pad, D_MODEL)[:T]


def setup_inputs(seed: int = 0) -> dict:
    key = jax.random.key(seed)
    ks = jax.random.split(key, 17)

    def normal(k, shape, scale):
        return jax.random.normal(k, shape, jnp.float32) * scale

    return {
        'x': normal(ks[0], (BATCH, SEQ, D_MODEL), 1.0),
        'meta_tokens': normal(ks[1], (N_META, D_MODEL), 1.0),
        'norm_mix_g': 1.0 + normal(ks[2], (DEPTH, D_MODEL), 0.02),
        'w_in': normal(ks[3], (DEPTH, D_MODEL, IN_WIDTH), D_MODEL ** -0.5),
        'b_in': normal(ks[4], (DEPTH, IN_WIDTH), 0.02),
        'attn_sinks': normal(ks[5], (DEPTH, N_Q_HEADS), 0.5),
        'lb_logits': normal(ks[6], (DEPTH + 1, REC_K_WIDTH), 0.5),
        'rec_norm_g': 1.0 + normal(ks[7], (DEPTH, REC_V_WIDTH), 0.02),
        'w_up_attn': normal(ks[8], (DEPTH, ATTN_WIDTH, D_MODEL), ATTN_WIDTH ** -0.5),
        'w_up_rec': normal(ks[9], (DEPTH, REC_V_WIDTH, D_MODEL), REC_V_WIDTH ** -0.5),
        'w_out': normal(ks[10], (DEPTH, D_MODEL, D_MODEL), D_MODEL ** -0.5),
        'norm_ffn_g': 1.0 + normal(ks[11], (DEPTH, D_MODEL), 0.02),
        'peer_w_query': normal(ks[12], (DEPTH, D_MODEL, PEER_HEADS * PEER_QUERY_DIM), D_MODEL ** -0.5),
        'peer_sub_keys': normal(ks[13], (DEPTH, PEER_HEADS, 2, N_SUBKEYS, PEER_QUERY_DIM // 2), (PEER_QUERY_DIM // 2) ** -0.5),
        'peer_expert_down': normal(ks[14], (DEPTH, N_EXPERTS, D_MODEL), D_MODEL ** -0.5),
        'peer_expert_up': normal(ks[15], (DEPTH, N_EXPERTS, D_MODEL), PEER_HEADS ** -0.5),
        'final_norm_g': 1.0 + normal(ks[16], (D_MODEL,), 0.02),
    }


def reference(x, meta_tokens, norm_mix_g, w_in, b_in, attn_sinks, lb_logits, rec_norm_g, w_up_attn, w_up_rec,
              w_out, norm_ffn_g, peer_w_query, peer_sub_keys, peer_expert_down, peer_expert_up, final_norm_g):
    B = x.shape[0]
    h = jnp.concatenate([jnp.broadcast_to(meta_tokens[None].astype(x.dtype), (B, N_META, D_MODEL)), x], axis=1)
    pos = jnp.arange(PAD_FRONT + h.shape[1]) - PAD_FRONT
    lower_bounds = jnp.cumsum(jax.nn.softmax(lb_logits.astype(jnp.float32), axis=0), axis=0)
    for layer in range(DEPTH):
        h = h + hybrid_mixer(rmsnorm(h, norm_mix_g[layer]), w_in[layer], b_in[layer], attn_sinks[layer],
                             lower_bounds[layer], rec_norm_g[layer], w_up_attn[layer], w_up_rec[layer],
                             w_out[layer], pos)
        if layer == DEPTH - 1:
            h = h[:, N_META:]
        L = h.shape[1]
        y = peer_ffn(rmsnorm(h, norm_ffn_g[layer]).reshape(B * L, D_MODEL), peer_w_query[layer],
                     peer_sub_keys[layer], peer_expert_down[layer], peer_expert_up[layer])
        h = h + y.reshape(B, L, D_MODEL)
    return rmsnorm(h, final_norm_g)
```

```python
import functools

import numpy as np
import jax
import jax.numpy as jnp
from jax import lax
from jax.experimental import pallas as pl
from jax.experimental.pallas import tpu as pltpu

D_MODEL = 1024
N_META = 16
BLOCK = 128
N_Q_HEADS = 8
N_KV_HEADS = 2
HEAD_DIM = 64
ROPE_DIM = HEAD_DIM // 4
ROPE_THETA = 500000.0
N_R_HEADS = 4
R_DIM = 128
REC_WIDTH = N_R_HEADS * R_DIM
ATTN_WIDTH = N_Q_HEADS * HEAD_DIM
KV_WIDTH = N_KV_HEADS * HEAD_DIM
N_SUBKEYS = 128
N_EXPERTS = N_SUBKEYS * N_SUBKEYS
PEER_HEADS = 8
PEER_HALF = 64
PEER_TOPK = 16
EPS = 1e-6
MASK_VALUE = -1e30

LANES = 128
CHUNK = 128
VMEM_LIMIT = 56 * 1024 * 1024

_OFF = np.cumsum([0, ATTN_WIDTH, KV_WIDTH, KV_WIDTH, REC_WIDTH, REC_WIDTH, REC_WIDTH, REC_WIDTH, D_MODEL, D_MODEL])
(_Q0, _K0, _V0, _RQ0, _RF0, _RI0, _RG0, _GA0, _GR0, _END) = [int(v) for v in _OFF]

bf16 = jnp.bfloat16
f32 = jnp.float32


def _params(sem):
    return pltpu.CompilerParams(dimension_semantics=sem, vmem_limit_bytes=VMEM_LIMIT)


def _rope(t, cos, sa, sb):
    return t * cos + pltpu.roll(t, LANES - ROPE_DIM // 2, 1) * sa + pltpu.roll(t, ROPE_DIM // 2, 1) * sb


def _dup_halves(t):
    lane = lax.broadcasted_iota(jnp.int32, t.shape, 1)
    sw = pltpu.roll(t, HEAD_DIM, 1)
    lo = lane < HEAD_DIM
    return jnp.where(lo, t, sw), jnp.where(lo, sw, t)


def _in_proj_kernel(x_ref, g_ref, w_ref, b_ref, cos_ref, sa_ref, sb_ref, lbl_ref,
                    q_ref, kd_ref, vd_ref, rq_ref, rk_ref, ri_ref, lf_ref, sg_ref, ga_ref, gr_ref):
    x = x_ref[...]
    ms = jnp.mean(x * x, axis=-1, keepdims=True)
    xn = (x * lax.rsqrt(ms + EPS) * g_ref[...]).astype(bf16)

    def proj(a, b):
        return jnp.dot(xn, w_ref[:, a:b], preferred_element_type=f32) + b_ref[:, a:b]

    cos, sa, sb = cos_ref[...], sa_ref[...], sb_ref[...]
    q = proj(_Q0, _K0)
    scale = HEAD_DIM ** -0.5
    for c in range(ATTN_WIDTH // LANES):
        q_ref[:, c * LANES:(c + 1) * LANES] = (_rope(q[:, c * LANES:(c + 1) * LANES], cos, sa, sb) * scale).astype(bf16)
    kv = proj(_K0, _RQ0)
    k0, k1 = _dup_halves(_rope(kv[:, :LANES], cos, sa, sb))
    kd_ref[:, :LANES] = k0.astype(bf16)
    kd_ref[:, LANES:] = k1.astype(bf16)
    v0, v1 = _dup_halves(kv[:, LANES:])
    vd_ref[:, :LANES] = v0.astype(bf16)
    vd_ref[:, LANES:] = v1.astype(bf16)

    rq_ref[...] = proj(_RQ0, _RF0).astype(bf16)
    lbl = lbl_ref[...]
    e = jnp.exp(lbl - jnp.max(lbl, axis=0, keepdims=True))
    lb = e[0:1, :] / jnp.sum(e, axis=0, keepdims=True)
    z = proj(_RF0, _RI0)
    lf_ref[...] = jnp.log(lb + (1.0 - lb) * jax.nn.sigmoid(z))
    rk_ref[...] = ((1.0 - lb) * jax.nn.sigmoid(-z)).astype(bf16)
    ri_ref[...] = proj(_RI0, _RG0).astype(bf16)
    rg = proj(_RG0, _GA0)
    sg_ref[...] = (rg * jax.nn.sigmoid(rg)).astype(bf16)
    ga_ref[...] = jax.nn.sigmoid(proj(_GA0, _GR0)).astype(bf16)
    gr_ref[...] = jax.nn.sigmoid(proj(_GR0, _END)).astype(bf16)


def _in_proj(x_rows, g, w, b, cos, sa, sb, lbl, tm):
    rows = x_rows.shape[0]
    nt = cos.shape[0] // tm
    row = lambda i: (i, 0)
    full = lambda i: (0, 0)
    tab = lambda i: (i % nt, 0)
    widths = [ATTN_WIDTH, 2 * KV_WIDTH, 2 * KV_WIDTH, REC_WIDTH, REC_WIDTH, REC_WIDTH, REC_WIDTH, REC_WIDTH,
              D_MODEL, D_MODEL]
    dtypes = [bf16, bf16, bf16, bf16, bf16, bf16, f32, bf16, bf16, bf16]
    return pl.pallas_call(
        _in_proj_kernel,
        grid=(rows // tm,),
        in_specs=[pl.BlockSpec((tm, D_MODEL), row), pl.BlockSpec((1, D_MODEL), full),
                  pl.BlockSpec((D_MODEL, _END), full), pl.BlockSpec((1, _END), full),
                  pl.BlockSpec((tm, LANES), tab), pl.BlockSpec((tm, LANES), tab), pl.BlockSpec((tm, LANES), tab),
                  pl.BlockSpec((2, REC_WIDTH), full)],
        out_specs=[pl.BlockSpec((tm, wd), row) for wd in widths],
        out_shape=[jax.ShapeDtypeStruct((rows, wd), dt) for wd, dt in zip(widths, dtypes)],
        compiler_params=_params(("parallel",)),
        name="in_proj",
    )(x_rows, g, w, b, cos, sa, sb, lbl)


def _rope_tables(pos):
    half = ROPE_DIM // 2
    inv_freq = jnp.power(ROPE_THETA, -jnp.arange(half, dtype=f32) * 2.0 / ROPE_DIM)
    ang = pos.astype(f32)[:, None] * inv_freq[None, :]
    cos, sin = jnp.cos(ang), jnp.sin(ang)
    n = pos.shape[0]
    pad = jnp.zeros((n, HEAD_DIM - ROPE_DIM), f32)
    zero = jnp.zeros((n, half), f32)
    c64 = jnp.concatenate([cos, cos, pad + 1.0], axis=1)
    a64 = jnp.concatenate([-sin, zero, pad], axis=1)
    b64 = jnp.concatenate([zero, sin, pad], axis=1)
    two = lambda t: jnp.concatenate([t, t], axis=1)
    return two(c64), two(a64), two(b64)


def _attn_kernel(sink_ref, q_ref, kc_ref, kp_ref, vc_ref, vp_ref, km_ref, vm_ref, o_ref, *, blocks_per_seq):
    j = pl.program_id(0) % blocks_per_seq
    row = lax.broadcasted_iota(jnp.int32, (BLOCK, 3 * BLOCK), 0)
    col = lax.broadcasted_iota(jnp.int32, (BLOCK, 3 * BLOCK), 1)
    prev_ok = (col < BLOCK) & (col > row) & (j > 0)
    cur_ok = (col >= BLOCK) & (col < 2 * BLOCK) & (col - BLOCK <= row)
    meta_ok = (col >= 2 * BLOCK) & (col < 2 * BLOCK + N_META)
    ok = prev_ok | cur_ok | meta_ok
    lane = lax.broadcasted_iota(jnp.int32, (BLOCK, LANES), 1)
    lo = lane < HEAD_DIM
    zpad = jnp.zeros((BLOCK - N_META, LANES), bf16)
    for g in range(N_KV_HEADS):
        sl = slice(g * LANES, (g + 1) * LANES)
        kcat = jnp.concatenate([kp_ref[:, sl], kc_ref[:, sl], km_ref[:, sl], zpad], axis=0)
        vcat = jnp.concatenate([vp_ref[:, sl], vc_ref[:, sl], vm_ref[:, sl], zpad], axis=0)
        for cc in range(2):
            c = 2 * g + cc
            qc = q_ref[:, c * LANES:(c + 1) * LANES]
            outs = []
            for hh in range(2):
                h = 2 * c + hh
                qh = jnp.where(lo if hh == 0 else ~lo, qc, jnp.zeros_like(qc))
                s = lax.dot_general(qh, kcat, (((1,), (1,)), ((), ())), preferred_element_type=f32)
                s = jnp.where(ok, s, MASK_VALUE)
                sink = sink_ref[h]
                m = jnp.maximum(jnp.max(s, axis=-1, keepdims=True), sink)
                p = jnp.exp(s - m)
                den = jnp.sum(p, axis=-1, keepdims=True) + jnp.exp(sink - m)
                o = jnp.dot(p.astype(bf16), vcat, preferred_element_type=f32)
                outs.append(o / den)
            o_ref[:, c * LANES:(c + 1) * LANES] = jnp.where(lo, outs[0], outs[1]).astype(bf16)


def _attn(sinks, q, kd, vd, kmeta, vmeta, blocks_per_seq):
    rows = q.shape[0]
    cur = lambda r: (r, 0)
    prev = lambda r: (jnp.maximum(r - 1, 0), 0)
    full = lambda r: (0, 0)
    return pl.pallas_call(
        functools.partial(_attn_kernel, blocks_per_seq=blocks_per_seq),
        grid=(rows // BLOCK,),
        in_specs=[pl.BlockSpec(memory_space=pltpu.SMEM),
                  pl.BlockSpec((BLOCK, ATTN_WIDTH), cur),
                  pl.BlockSpec((BLOCK, 2 * KV_WIDTH), cur), pl.BlockSpec((BLOCK, 2 * KV_WIDTH), prev),
                  pl.BlockSpec((BLOCK, 2 * KV_WIDTH), cur), pl.BlockSpec((BLOCK, 2 * KV_WIDTH), prev),
                  pl.BlockSpec((N_META, 2 * KV_WIDTH), full), pl.BlockSpec((N_META, 2 * KV_WIDTH), full)],
        out_specs=pl.BlockSpec((BLOCK, ATTN_WIDTH), cur),
        out_shape=jax.ShapeDtypeStruct((rows, ATTN_WIDTH), bf16),
        compiler_params=_params(("parallel",)),
        name="attn",
    )(sinks, q, kd, kd, vd, vd, kmeta, vmeta)


_LEVELS = [CHUNK >> (i + 1) for i in range(int(np.log2(CHUNK)))]


def _hgrn_tables():
    t = np.arange(CHUNK)[:, None]
    u = np.arange(CHUNK)[None, :]
    groups = [(u <= t), (u > t)]
    masks = [(t == u)]
    for m in _LEVELS:
        r = (t // (2 * m)) * (2 * m) + m - 1
        upper = (t % (2 * m)) >= m
        groups.append(np.where(upper, (u > r) & (u <= t), (u > t) & (u <= r)))
        masks.append(upper & ((u % (2 * m)) < m) & ((t // (2 * m)) == (u // (2 * m))))
    seg = np.concatenate(groups, axis=0).astype(np.float32)
    msk = np.stack(masks, axis=0).astype(np.float32)
    return seg, msk


def _hgrn_kernel(seg_ref, msk_ref, s0_ref, rq_ref, rk_ref, ri_ref, lf_ref, sg_ref, g_ref,
                 o_ref, sout_ref, st_ref, *, chunks):
    @pl.when(pl.program_id(1) == 0)
    def _():
        st_ref[...] = s0_ref[...]

    seg = seg_ref[...]
    nt = (((1,), (1,)), ((), ()))

    def chunk_body(ci, carry):
        rows = pl.ds(pl.multiple_of(ci * CHUNK, CHUNK), CHUNK)
        lf = lf_ref[rows, :]
        hi = lf.astype(bf16)
        lo = (lf - hi.astype(f32)).astype(bf16)
        dec = jnp.exp(jnp.dot(seg, hi, preferred_element_type=f32) + jnp.dot(seg, lo, preferred_element_type=f32))
        for h in range(N_R_HEADS):
            hs = slice(h * R_DIM, (h + 1) * R_DIM)
            q = rq_ref[rows, hs].astype(f32)
            k = rk_ref[rows, hs].astype(f32)
            v = ri_ref[rows, hs]
            x = lambda grp: dec[grp * CHUNK:(grp + 1) * CHUNK, hs]
            st = st_ref[h]
            o = lax.dot_general((q * x(0)).astype(bf16), st.astype(bf16), nt, preferred_element_type=f32)
            a = msk_ref[0] * lax.dot_general(q.astype(bf16), k.astype(bf16), nt, preferred_element_type=f32)
            for l in range(len(_LEVELS)):
                xl = x(2 + l)
                a = a + msk_ref[1 + l] * lax.dot_general((q * xl).astype(bf16), (k * xl).astype(bf16), nt,
                                                         preferred_element_type=f32)
            o = o + jnp.dot(a.astype(bf16), v, preferred_element_type=f32)
            vt = v.astype(f32).T.astype(bf16)
            decay = x(0)[CHUNK - 1:CHUNK, :]
            st_ref[h] = st * decay + jnp.dot(vt, (k * x(1)).astype(bf16), preferred_element_type=f32)
            ms = jnp.mean(o * o, axis=-1, keepdims=True)
            y = o * lax.rsqrt(ms + EPS) * g_ref[:, hs] * sg_ref[rows, hs].astype(f32)
            o_ref[rows, hs] = y.astype(bf16)
        return carry

    lax.fori_loop(0, chunks, chunk_body, 0)
    sout_ref[0] = st_ref[...]


def _hgrn(seg, msk, s0, rq, rk, ri, lf, sg, g, nseq, tr):
    rows = rq.shape[0]
    nst = rows // (nseq * tr)
    row = lambda b, s: (b * nst + s, 0)
    c2 = lambda b, s: (0, 0)
    c3 = lambda b, s: (0, 0, 0)
    ng = seg.shape[0] // CHUNK
    return pl.pallas_call(
        functools.partial(_hgrn_kernel, chunks=tr // CHUNK),
        grid=(nseq, nst),
        in_specs=[pl.BlockSpec((ng * CHUNK, CHUNK), c2), pl.BlockSpec((ng - 1, CHUNK, CHUNK), c3),
                  pl.BlockSpec((N_R_HEADS, R_DIM, R_DIM), c3),
                  pl.BlockSpec((tr, REC_WIDTH), row), pl.BlockSpec((tr, REC_WIDTH), row),
                  pl.BlockSpec((tr, REC_WIDTH), row), pl.BlockSpec((tr, REC_WIDTH), row),
                  pl.BlockSpec((tr, REC_WIDTH), row), pl.BlockSpec((1, REC_WIDTH), c2)],
        out_specs=[pl.BlockSpec((tr, REC_WIDTH), row),
                   pl.BlockSpec((1, N_R_HEADS, R_DIM, R_DIM), lambda b, s: (b, 0, 0, 0))],
        out_shape=[jax.ShapeDtypeStruct((rows, REC_WIDTH), bf16),
                   jax.ShapeDtypeStruct((nseq, N_R_HEADS, R_DIM, R_DIM), f32)],
        scratch_shapes=[pltpu.VMEM((N_R_HEADS, R_DIM, R_DIM), f32)],
        compiler_params=_params(("arbitrary", "arbitrary")),
        name="hgrn",
    )(seg, msk, s0, rq, rk, ri, lf, sg, g)


def _merge_kernel(x_ref, at_ref, rc_ref, ga_ref, gr_ref, wa_ref, wr_ref, wo_ref, g_ref, wq_ref, key_ref,
                  h_ref, xn_ref, sc_ref):
    ua = jnp.dot(at_ref[...], wa_ref[...], preferred_element_type=f32)
    ur = jnp.dot(rc_ref[...], wr_ref[...], preferred_element_type=f32)
    merged = ga_ref[...].astype(f32) * ua + gr_ref[...].astype(f32) * ur
    h = x_ref[...] + jnp.dot(merged.astype(bf16), wo_ref[...], preferred_element_type=f32)
    h_ref[...] = h
    ms = jnp.mean(h * h, axis=-1, keepdims=True)
    xn = (h * lax.rsqrt(ms + EPS) * g_ref[...]).astype(bf16)
    xn_ref[...] = xn
    qry = jnp.dot(xn, wq_ref[...], preferred_element_type=f32).astype(bf16)
    for hp in range(2 * PEER_HEADS):
        c = hp // 2
        sc_ref[hp] = lax.dot_general(key_ref[hp], qry[:, c * LANES:(c + 1) * LANES], (((1,), (1,)), ((), ())),
                                     preferred_element_type=f32)


def _merge(x_rows, attn, rec, ga, gr, wa, wr, wo, g, wq, keys, tm):
    rows = x_rows.shape[0]
    row = lambda i: (i, 0)
    c2 = lambda i: (0, 0)
    c3 = lambda i: (0, 0, 0)
    nhp = 2 * PEER_HEADS
    return pl.pallas_call(
        _merge_kernel,
        grid=(rows // tm,),
        in_specs=[pl.BlockSpec((tm, D_MODEL), row), pl.BlockSpec((tm, ATTN_WIDTH), row),
                  pl.BlockSpec((tm, REC_WIDTH), row), pl.BlockSpec((tm, D_MODEL), row),
                  pl.BlockSpec((tm, D_MODEL), row),
                  pl.BlockSpec((ATTN_WIDTH, D_MODEL), c2), pl.BlockSpec((REC_WIDTH, D_MODEL), c2),
                  pl.BlockSpec((D_MODEL, D_MODEL), c2), pl.BlockSpec((1, D_MODEL), c2),
                  pl.BlockSpec((D_MODEL, D_MODEL), c2), pl.BlockSpec((nhp, N_SUBKEYS, LANES), c3)],
        out_specs=[pl.BlockSpec((tm, D_MODEL), row), pl.BlockSpec((tm, D_MODEL), row),
                   pl.BlockSpec((nhp, N_SUBKEYS, tm), lambda i: (0, 0, i))],
        out_shape=[jax.ShapeDtypeStruct((rows, D_MODEL), f32), jax.ShapeDtypeStruct((rows, D_MODEL), bf16),
                   jax.ShapeDtypeStruct((nhp, N_SUBKEYS, rows), f32)],
        compiler_params=_params(("parallel",)),
        name="merge",
    )(x_rows, attn, rec, ga, gr, wa, wr, wo, g, wq, keys)


def _extract_top(s):
    n, w = s.shape
    idx = lax.broadcasted_iota(jnp.int32, (n, w), 0).astype(f32)
    kio = lax.broadcasted_iota(jnp.int32, (PEER_TOPK, w), 0)

    def body(k, carry):
        s, rank, top = carry
        m = jnp.max(s, axis=0, keepdims=True)
        first = jnp.min(jnp.where(s == m, idx, float(n)), axis=0, keepdims=True)
        sel = idx == first
        s = jnp.where(sel, -jnp.inf, s)
        rank = jnp.where(sel, k.astype(f32), rank)
        top = jnp.where(kio == k, m, top)
        return s, rank, top

    init = (s, jnp.full((n, w), float(PEER_TOPK), f32), jnp.zeros((PEER_TOPK, w), f32))
    _, rank, top = lax.fori_loop(0, PEER_TOPK, body, init)
    return rank, top


def _staircase(a, b):
    kk, w = a.shape
    kio = lax.broadcasted_iota(jnp.int32, (kk, w), 0).astype(f32)
    front = a + b[0:1, :]
    top = front[0:1, :]

    def body(_, carry):
        front, height, z = carry
        m = jnp.max(front, axis=0, keepdims=True)
        k1 = jnp.min(jnp.where(front == m, kio, float(kk)), axis=0, keepdims=True)
        sel = kio == k1
        z = z + jnp.exp(m - top)
        nxt = jnp.max(jnp.where(sel, height, -1.0), axis=0, keepdims=True) + 1.0
        height = jnp.where(sel, nxt, height)
        bnext = jnp.max(jnp.where(kio == nxt, b, -jnp.inf), axis=0, keepdims=True)
        front = jnp.where(sel, a + bnext, front)
        return front, height, z

    init = (front, jnp.zeros((kk, w), f32), jnp.zeros((1, w), f32))
    _, height, z = lax.fori_loop(0, PEER_TOPK, body, init)
    return height, z


def _topk_kernel(sc_ref, c1_ref, w1_ref, r2_ref, w2_ref):
    def head_body(h, carry):
        s1 = sc_ref[2 * h]
        s2 = sc_ref[2 * h + 1]
        r1, a = _extract_top(s1)
        r2, b = _extract_top(s2)
        height, z = _staircase(a, b)
        c1 = jnp.zeros_like(r1)
        for k in range(PEER_TOPK):
            c1 = jnp.where(r1 == float(k), height[k:k + 1, :], c1)
        c1_ref[h] = c1
        w1_ref[h] = jnp.exp(s1 - a[0:1, :]) * (1.0 / z)
        r2_ref[h] = r2
        w2_ref[h] = jnp.exp(s2 - b[0:1, :])
        return carry

    lax.fori_loop(0, PEER_HEADS, head_body, 0)


def _topk(scores_t, tq):
    rows = scores_t.shape[-1]
    spec = pl.BlockSpec((PEER_HEADS, N_SUBKEYS, tq), lambda i: (0, 0, i))
    shp = jax.ShapeDtypeStruct((PEER_HEADS, N_SUBKEYS, rows), f32)
    return pl.pallas_call(
        _topk_kernel,
        grid=(rows // tq,),
        in_specs=[pl.BlockSpec((2 * PEER_HEADS, N_SUBKEYS, tq), lambda i: (0, 0, i))],
        out_specs=[spec, spec, spec, spec],
        out_shape=[shp, shp, shp, shp],
        compiler_params=_params(("parallel",)),
        name="topk",
    )(scores_t)


def _peer_kernel(xn_ref, dn_ref, up_ref, c1_ref, w1_ref, r2_ref, w2_ref, h_ref, g_ref, o_ref, acc_ref, hid_ref,
                 *, slabs):
    j = pl.program_id(1)

    @pl.when(j == 0)
    def _():
        acc_ref[...] = jnp.zeros_like(acc_ref)

    p = lax.dot_general(dn_ref[...], xn_ref[...], (((1,), (1,)), ((), ())), preferred_element_type=f32)
    for s in range(slabs):
        ps = p[s * N_SUBKEYS:(s + 1) * N_SUBKEYS, :]
        act = 0.5 * ps * (1.0 + lax.erf(ps * (2.0 ** -0.5)))
        wt = jnp.zeros_like(ps)
        for h in range(PEER_HEADS):
            keep = r2_ref[h] < c1_ref[h, s:s + 1, :]
            wt = wt + w1_ref[h, s:s + 1, :] * jnp.where(keep, w2_ref[h], 0.0)
        hid_ref[s * N_SUBKEYS:(s + 1) * N_SUBKEYS, :] = (act * wt).astype(bf16)
    acc_ref[...] += jnp.dot(up_ref[...], hid_ref[...], preferred_element_type=f32)

    @pl.when(j == pl.num_programs(1) - 1)
    def _():
        hh = h_ref[...] + acc_ref[...].T
        ms = jnp.mean(hh * hh, axis=-1, keepdims=True)
        o_ref[...] = hh * lax.rsqrt(ms + EPS) * g_ref[...]


def _peer(xn, down, up_t, c1, w1, r2, w2, h, g, tm, te):
    rows = xn.shape[0]
    slabs = te // N_SUBKEYS
    tok = lambda i, j: (i, 0)
    sel1 = lambda i, j: (0, j, i)
    sel2 = lambda i, j: (0, 0, i)
    return pl.pallas_call(
        functools.partial(_peer_kernel, slabs=slabs),
        grid=(rows // tm, N_EXPERTS // te),
        in_specs=[pl.BlockSpec((tm, D_MODEL), tok),
                  pl.BlockSpec((te, D_MODEL), lambda i, j: (j, 0)),
                  pl.BlockSpec((D_MODEL, te), lambda i, j: (0, j)),
                  pl.BlockSpec((PEER_HEADS, slabs, tm), sel1), pl.BlockSpec((PEER_HEADS, slabs, tm), sel1),
                  pl.BlockSpec((PEER_HEADS, N_SUBKEYS, tm), sel2), pl.BlockSpec((PEER_HEADS, N_SUBKEYS, tm), sel2),
                  pl.BlockSpec((tm, D_MODEL), tok), pl.BlockSpec((1, D_MODEL), lambda i, j: (0, 0))],
        out_specs=pl.BlockSpec((tm, D_MODEL), tok),
        out_shape=jax.ShapeDtypeStruct((rows, D_MODEL), f32),
        scratch_shapes=[pltpu.VMEM((D_MODEL, tm), f32), pltpu.VMEM((te, tm), bf16)],
        compiler_params=_params(("parallel", "arbitrary")),
        name="peer",
    )(xn, down, up_t, c1, w1, r2, w2, h, g)


def kernel(x, meta_tokens, norm_mix_g, w_in, b_in, attn_sinks, lb_logits, rec_norm_g, w_up_attn, w_up_rec, w_out,
           norm_ffn_g, peer_w_query, peer_sub_keys, peer_expert_down, peer_expert_up, final_norm_g):
    nseq, seq, d = x.shape
    rows = nseq * seq
    x_rows = x.reshape(rows, d)

    w_in_b = w_in[0].astype(bf16)
    b_in_r = b_in[0].reshape(1, -1)
    g_mix = norm_mix_g[0].reshape(1, -1)
    lbl = lb_logits.astype(f32)

    cm, am, bm = _rope_tables(jnp.arange(N_META))
    meta = _in_proj(meta_tokens.astype(f32), g_mix, w_in_b, b_in_r, cm, am, bm, lbl, N_META)
    _, kd_m, vd_m, rq_m, rk_m, ri_m, lf_m, sg_m, _, _ = meta
    cx, ax, bx = _rope_tables(N_META + jnp.arange(seq))
    q, kd, vd, rq, rk, ri, lf, sg, ga, gr = _in_proj(x_rows, g_mix, w_in_b, b_in_r, cx, ax, bx, lbl, 256)

    attn = _attn(attn_sinks[0].astype(f32), q, kd, vd, kd_m, vd_m, seq // BLOCK)

    seg_np, msk_np = _hgrn_tables()
    seg = jnp.asarray(seg_np, bf16)
    msk = jnp.asarray(msk_np, f32)
    g_rec = rec_norm_g[0].reshape(1, -1).astype(f32)
    padm = lambda t: jnp.pad(t, ((0, CHUNK - N_META), (0, 0)))
    zero_state = jnp.zeros((N_R_HEADS, R_DIM, R_DIM), f32)
    _, s_meta = _hgrn(seg, msk, zero_state, padm(rq_m), padm(rk_m), padm(ri_m), padm(lf_m), padm(sg_m), g_rec,
                      1, CHUNK)
    rec, _ = _hgrn(seg, msk, s_meta[0], rq, rk, ri, lf, sg, g_rec, nseq, 512)

    keys = peer_sub_keys[0].reshape(2 * PEER_HEADS, N_SUBKEYS, PEER_HALF).astype(bf16)
    zk = jnp.zeros_like(keys)
    even = (jnp.arange(2 * PEER_HEADS) % 2 == 0)[:, None, None]
    keys_pad = jnp.concatenate([jnp.where(even, keys, zk), jnp.where(even, zk, keys)], axis=-1)
    h1, xn2, scores_t = _merge(x_rows, attn, rec, ga, gr, w_up_attn[0].astype(bf16), w_up_rec[0].astype(bf16),
                               w_out[0].astype(bf16), norm_ffn_g[0].reshape(1, -1), peer_w_query[0].astype(bf16),
                               keys_pad, 256)

    c1, w1, r2, w2 = _topk(scores_t, LANES)
    out = _peer(xn2, peer_expert_down[0].astype(bf16), peer_expert_up[0].T.astype(bf16), c1, w1, r2, w2, h1,
                final_norm_g.reshape(1, -1), 512, 1024)
    return out.reshape(nseq, seq, d)
```
